```python
import jax, jax.numpy as jnp
from jax import lax
import numpy as np

D_MODEL = 1024
BATCH = 8
SEQ = 2048
DEPTH = 1
DEC_BATCH = 128
DEC_SEQ = 8
PAST_LEN = 16384
PAGE_SIZE = 128

MIX_WIDTH = D_MODEL
CONV_CH = MIX_WIDTH // 2
CONV_HEADS = 8
CONV_K = 3
POOL_CH = MIX_WIDTH - CONV_CH
POOL_WINDOWS = (2, 4, 8, 16)
POOL_GROUPS = len(POOL_WINDOWS)
POOL_GROUP_CH = POOL_CH // POOL_GROUPS
POOL_BUF = max(POOL_WINDOWS) - 1
IN_COLS = 3 * CONV_CH + POOL_CH
MEM_TOKENS = 256
MEM_HEADS = 4
MEM_HEAD_DIM = D_MODEL // MEM_HEADS
PEER_HEADS = 8
PEER_NKEYS = 128
PEER_NEXPERTS = PEER_NKEYS * PEER_NKEYS
PEER_TOPK = 16
PEER_KEY_DIM = 256
PEER_HALF = PEER_KEY_DIM // 2
PEER_CHUNK = 128
EPS = 1e-6

kernel_name = 'hymba_conv_pool_peer_decoder_step'


def rmsnorm(x, g):
    xf = x.astype(jnp.float32)
    r = lax.rsqrt(jnp.mean(xf * xf, axis=-1, keepdims=True) + EPS)
    return (xf * r).astype(x.dtype) * g


def short_conv(z, buf, w):
    S = z.shape[1]
    zp = jnp.concatenate([buf.astype(z.dtype), z], axis=1)
    y = w[0] * zp[:, 0:S] + w[1] * zp[:, 1:S + 1] + w[2] * zp[:, 2:S + 2]
    return y, zp[:, -(CONV_K - 1):]


def multi_scale_pool(p, buf, w_pool, scale, start_pos):
    B, S, C = p.shape
    pc = jnp.concatenate([buf.astype(p.dtype), p], axis=1)
    pf = pc.astype(jnp.float32)
    cs = jnp.concatenate([jnp.zeros((B, 1, C), jnp.float32), jnp.cumsum(pf, axis=1)], axis=1)
    end = cs[:, POOL_BUF + 1:POOL_BUF + 1 + S]
    pos = jnp.arange(S, dtype=jnp.int32) + start_pos
    outs = []
    for gi, w in enumerate(POOL_WINDOWS):
        sl = slice(gi * POOL_GROUP_CH, (gi + 1) * POOL_GROUP_CH)
        begin = cs[:, POOL_BUF + 1 - w:POOL_BUF + 1 - w + S, sl]
        cnt = jnp.minimum(pos + 1, w).astype(jnp.float32)[None, :, None]
        outs.append((end[..., sl] - begin) / cnt - pf[:, POOL_BUF:, sl])
    d = jnp.concatenate(outs, axis=-1).astype(p.dtype).reshape(B, S, POOL_GROUPS, POOL_GROUP_CH)
    y = jnp.einsum('bsgc,gcd->bsgd', d, w_pool).reshape(B, S, C) * scale
    return y, pc[:, -POOL_BUF:]


def parallel_mixer(x, conv_buf, pool_buf, start_pos, g, w_in, conv_w, pool_w, pool_scale, w_out):
    h = rmsnorm(x, g)
    proj = h @ w_in
    b = proj[..., :CONV_CH]
    c = proj[..., CONV_CH:2 * CONV_CH]
    hc = proj[..., 2 * CONV_CH:3 * CONV_CH]
    p = proj[..., 3 * CONV_CH:]
    zc, new_conv = short_conv(c * hc, conv_buf, conv_w)
    a_out = b * zc
    p_out, new_pool = multi_scale_pool(p, pool_buf, pool_w, pool_scale, start_pos)
    y = jnp.concatenate([a_out, p_out], axis=-1) @ w_out
    return x + y, new_conv, new_pool


def mem_kv(mem, mem_g, w_mk, w_mv):
    B, M, _ = mem.shape
    mn = rmsnorm(mem, mem_g)
    k = (mn @ w_mk).reshape(B, M, MEM_HEADS, MEM_HEAD_DIM)
    v = (mn @ w_mv).reshape(B, M, MEM_HEADS, MEM_HEAD_DIM)
    return k, v


def memory_cross_attn(x, k, v, g, w_mq, w_mo):
    B, S, D = x.shape
    h = rmsnorm(x, g)
    q = (h @ w_mq).reshape(B, S, MEM_HEADS, MEM_HEAD_DIM)
    s = jnp.einsum('bshd,bmhd->bhsm', q, k.astype(q.dtype)).astype(jnp.float32) * (MEM_HEAD_DIM ** -0.5)
    pr = jax.nn.softmax(s, axis=-1).astype(x.dtype)
    o = jnp.einsum('bhsm,bmhd->bshd', pr, v.astype(x.dtype)).reshape(B, S, D)
    return x + o @ w_mo


def peer_chunk(t, wq, keys, u, v):
    C = t.shape[0]
    q = (t @ wq).reshape(C, PEER_HEADS, 2, PEER_HALF)
    s = jnp.einsum('chpd,pnd->chpn', q, keys).astype(jnp.float32)
    sv, si = lax.top_k(s, PEER_TOPK)
    cand = (sv[:, :, 0, :, None] + sv[:, :, 1, None, :]).reshape(C, PEER_HEADS, PEER_TOPK * PEER_TOPK)
    cid = (si[:, :, 0, :, None] * PEER_NKEYS + si[:, :, 1, None, :]).reshape(C, PEER_HEADS, PEER_TOPK * PEER_TOPK)
    fv, fi = lax.top_k(cand, PEER_TOPK)
    eid = jnp.take_along_axis(cid, fi, axis=-1)
    gate = jax.nn.softmax(fv, axis=-1).astype(t.dtype)
    a = jnp.einsum('chkd,cd->chk', u[eid], t)
    act = gate * jax.nn.gelu(a, approximate=False)
    return jnp.einsum('chk,chkd->cd', act, v[eid])


def peer_ffn(x, g, wq, keys, u, v):
    B, S, D = x.shape
    t = rmsnorm(x, g).reshape(B * S, D)
    T = B * S
    pad = (-T) % PEER_CHUNK
    t = jnp.pad(t, ((0, pad), (0, 0))).reshape(-1, PEER_CHUNK, D)
    y = lax.map(lambda c: peer_chunk(c, wq, keys, u, v), t)
    y = y.reshape(-1, D)[:T].reshape(B, S, D)
    return x + y


def layer(x, conv_buf, pool_buf, mk, mv, start_pos, lp):
    (norm_mix_g, w_in, conv_w, pool_w, pool_scale, w_out, norm_mem_g, w_mq, w_mo,
     norm_ffn_g, peer_wq, peer_keys, peer_u, peer_v) = lp
    x, new_conv, new_pool = parallel_mixer(x, conv_buf, pool_buf, start_pos, norm_mix_g, w_in,
                                           conv_w, pool_w, pool_scale, w_out)
    x = memory_cross_attn(x, mk, mv, norm_mem_g, w_mq, w_mo)
    x = peer_ffn(x, norm_ffn_g, peer_wq, peer_keys, peer_u, peer_v)
    return x, new_conv, new_pool


def setup_inputs(seed: int = 0) -> dict:
    key = jax.random.key(seed)
    ks = jax.random.split(key, 32)
    f32 = jnp.float32
    D = D_MODEL

    def nrm(k, shape, scale):
        return jax.random.normal(k, shape, f32) * scale

    def gain(k, n):
        return 1.0 + 0.05 * jax.random.normal(k, (DEPTH, n), f32)

    return {
        'x_prompt': nrm(ks[0], (BATCH, SEQ, D), 1.0),
        'x_sample': nrm(ks[1], (DEC_BATCH, DEC_SEQ, D), 1.0),
        'mem_prompt': nrm(ks[2], (BATCH, MEM_TOKENS, D), 1.0),
        'state_conv': nrm(ks[3], (DEPTH, DEC_BATCH, CONV_K - 1, CONV_CH), 1.0),
        'state_pool': nrm(ks[4], (DEPTH, DEC_BATCH, POOL_BUF, POOL_CH), 1.0),
        'cache_mem_k': nrm(ks[5], (DEPTH, DEC_BATCH, MEM_TOKENS, MEM_HEADS, MEM_HEAD_DIM), 1.0),
        'cache_mem_v': nrm(ks[6], (DEPTH, DEC_BATCH, MEM_TOKENS, MEM_HEADS, MEM_HEAD_DIM), 1.0),
        'norm_mix_g': gain(ks[7], D),
        'w_in': nrm(ks[8], (DEPTH, D, IN_COLS), D ** -0.5),
        'conv_w': nrm(ks[9], (DEPTH, CONV_K, CONV_CH), CONV_K ** -0.5),
        'pool_w': nrm(ks[10], (DEPTH, POOL_GROUPS, POOL_GROUP_CH, POOL_GROUP_CH), POOL_GROUP_CH ** -0.5),
        'pool_scale': gain(ks[11], POOL_CH),
        'w_out': nrm(ks[12], (DEPTH, MIX_WIDTH, D), MIX_WIDTH ** -0.5),
        'norm_mem_g': gain(ks[13], D),
        'mem_norm_g': gain(ks[14], D),
        'w_mq': nrm(ks[15], (DEPTH, D, D), D ** -0.5),
        'w_mk': nrm(ks[16], (DEPTH, D, D), D ** -0.5),
        'w_mv': nrm(ks[17], (DEPTH, D, D), D ** -0.5),
        'w_mo': nrm(ks[18], (DEPTH, D, D), D ** -0.5),
        'norm_ffn_g': gain(ks[19], D),
        'peer_wq': nrm(ks[20], (DEPTH, D, PEER_HEADS * PEER_KEY_DIM), D ** -0.5),
        'peer_keys': nrm(ks[21], (DEPTH, 2, PEER_NKEYS, PEER_HALF), PEER_HALF ** -0.5),
        'peer_u': nrm(ks[22], (DEPTH, PEER_NEXPERTS, D), D ** -0.5),
        'peer_v': nrm(ks[23], (DEPTH, PEER_NEXPERTS, D), PEER_HEADS ** -0.5),
        'final_g': 1.0 + 0.05 * jax.random.normal(ks[24], (D,), f32),
    }


def reference(x_prompt, x_sample, mem_prompt, state_conv, state_pool, cache_mem_k, cache_mem_v,
              norm_mix_g, w_in, conv_w, pool_w, pool_scale, w_out, norm_mem_g, mem_norm_g,
              w_mq, w_mk, w_mv, w_mo, norm_ffn_g, peer_wq, peer_keys, peer_u, peer_v, final_g):
    xp, xs = x_prompt, x_sample
    conv_p, pool_p, mk_p_all, mv_p_all, conv_s, pool_s = [], [], [], [], [], []
    for l in range(DEPTH):
        lp = (norm_mix_g[l], w_in[l], conv_w[l], pool_w[l], pool_scale[l], w_out[l], norm_mem_g[l],
              w_mq[l], w_mo[l], norm_ffn_g[l], peer_wq[l], peer_keys[l], peer_u[l], peer_v[l])
        mk_p, mv_p = mem_kv(mem_prompt, mem_norm_g[l], w_mk[l], w_mv[l])
        zc = jnp.zeros((xp.shape[0], CONV_K - 1, CONV_CH), xp.dtype)
        zp = jnp.zeros((xp.shape[0], POOL_BUF, POOL_CH), xp.dtype)
        xp, nc_p, np_p = layer(xp, zc, zp, mk_p, mv_p, 0, lp)
        xs, nc_s, np_s = layer(xs, state_conv[l], state_pool[l], cache_mem_k[l], cache_mem_v[l], PAST_LEN, lp)
        conv_p.append(nc_p)
        pool_p.append(np_p)
        mk_p_all.append(mk_p)
        mv_p_all.append(mv_p)
        conv_s.append(nc_s)
        pool_s.append(np_s)
    y_prompt = rmsnorm(xp, final_g)
    y_sample = rmsnorm(xs, final_g)
    return (y_prompt, y_sample, jnp.stack(conv_p), jnp.stack(pool_p), jnp.stack(mk_p_all),
            jnp.stack(mv_p_all), jnp.stack(conv_s), jnp.stack(pool_s))
```

```python
import functools

import jax
import jax.numpy as jnp
from jax import lax
from jax.experimental import pallas as pl
from jax.experimental.pallas import tpu as pltpu

F32 = jnp.float32
BF16 = jnp.bfloat16

EPS = 1e-6
CONV_K = 3
POOL_WINDOWS = (2, 4, 8, 16)
POOL_BUF = max(POOL_WINDOWS) - 1
MEM_HEADS = 4
PEER_HEADS = 8
PEER_TOPK = 16
PAST_LEN = 16384
ROUTE_ROUNDS = PEER_TOPK
UNRANKED = 64.0
LANES = 128
SUBLANES = 8
VMEM_LIMIT_BYTES = 56 * 1024 * 1024


def _rmsnorm(x, g):
    r = lax.rsqrt(jnp.mean(x * x, axis=-1, keepdims=True) + EPS)
    return (x * r) * g


def _dot(a, b):
    return jnp.dot(a, b, preferred_element_type=F32)


def _memkv_kernel(mem_ref, g_ref, wk_ref, wv_ref, k_ref, v_ref):
    mn = _rmsnorm(mem_ref[...], g_ref[...]).astype(BF16)
    k_ref[...] = _dot(mn, wk_ref[...])
    v_ref[...] = _dot(mn, wv_ref[...])


def _mem_kv(mem2d, g, wk, wv, rows):
    m, d = mem2d.shape
    return pl.pallas_call(
        _memkv_kernel,
        out_shape=(jax.ShapeDtypeStruct((m, d), F32), jax.ShapeDtypeStruct((m, d), F32)),
        grid=(m // rows,),
        in_specs=[pl.BlockSpec((rows, d), lambda i: (i, 0)),
                  pl.BlockSpec((1, d), lambda i: (0, 0)),
                  pl.BlockSpec((d, d), lambda i: (0, 0)),
                  pl.BlockSpec((d, d), lambda i: (0, 0))],
        out_specs=(pl.BlockSpec((rows, d), lambda i: (i, 0)),
                   pl.BlockSpec((rows, d), lambda i: (i, 0))),
        compiler_params=pltpu.CompilerParams(dimension_semantics=("arbitrary",),
                                             vmem_limit_bytes=VMEM_LIMIT_BYTES),
        name="mem_kv",
    )(mem2d, g, wk, wv)


def _mix_attn_kernel(x_ref, sc_ref, sp_ref, k_ref, v_ref, gmix_ref, win_ref, convw_ref, poolw_ref,
                     pscale_ref, wout_ref, gmem_ref, wq_ref, wo_ref,
                     x2_ref, nconv_ref, npool_ref, zbuf, pbuf, *, pos_base, seq_tiles):
    j = pl.program_id(1)
    ns, l, d = x_ref.shape
    m = ns * l
    cc = zbuf.shape[-1]
    pc = pbuf.shape[-1]

    @pl.when(j == 0)
    def _():
        zbuf[:, 0:8, :] = sc_ref[...]
        pbuf[:, 0:16, :] = sp_ref[...]

    if seq_tiles > 1:
        @pl.when(j > 0)
        def _():
            zbuf[:, 0:8, :] = zbuf[:, l:l + 8, :]
            pbuf[:, 0:16, :] = pbuf[:, l:l + 16, :]

    x = x_ref[...].reshape(m, d)
    h = _rmsnorm(x, gmix_ref[...]).astype(BF16)
    proj = _dot(h, win_ref[...])
    b = proj[:, 0:cc]
    z = proj[:, cc:2 * cc] * proj[:, 2 * cc:3 * cc]
    p = proj[:, 3 * cc:3 * cc + pc]
    zbuf[:, 8:8 + l, :] = z.reshape(ns, l, cc)
    pbuf[:, 16:16 + l, :] = p.reshape(ns, l, pc)

    cw = convw_ref[...]
    conv = (cw[0:1, :] * zbuf[:, 6:6 + l, :] + cw[1:2, :] * zbuf[:, 7:7 + l, :]
            + cw[2:3, :] * zbuf[:, 8:8 + l, :])
    a_out = b * conv.reshape(m, cc)

    gch = pc // len(POOL_WINDOWS)
    pos = pos_base + j * l + lax.broadcasted_iota(jnp.int32, (1, l, 1), 1)
    p_parts = []
    for gi, w in enumerate(POOL_WINDOWS):
        lo = gi * gch
        cur = pbuf[:, 16:16 + l, lo:lo + gch]
        s = cur
        for kk in range(1, w):
            s = s + pbuf[:, 16 - kk:16 - kk + l, lo:lo + gch]
        inv = 1.0 / jnp.minimum(pos + 1, w).astype(F32)
        dgrp = (s * inv - cur).reshape(m, gch).astype(BF16)
        p_parts.append(_dot(dgrp, poolw_ref[gi]))
    p_out = jnp.concatenate(p_parts, axis=-1) * pscale_ref[...]

    x1 = x + _dot(a_out.astype(BF16), wout_ref[0:cc, :]) + _dot(p_out.astype(BF16), wout_ref[cc:cc + pc, :])

    nconv_ref[...] = zbuf[:, l + 6:l + 8, :]
    npool_ref[...] = pbuf[:, l + 1:l + 16, :]

    hd = d // MEM_HEADS
    h2 = _rmsnorm(x1, gmem_ref[...]).astype(BF16)
    q = (_dot(h2, wq_ref[...]) * (hd ** -0.5)).astype(BF16).reshape(ns, l, d)
    x2_ref[...] = x1.reshape(ns, l, d)
    attn = None
    for hh in range(MEM_HEADS):
        cs = slice(hh * hd, (hh + 1) * hd)
        qh = q[:, :, cs]
        kh = k_ref[:, :, cs].astype(BF16)
        vh = v_ref[:, :, cs].astype(BF16)
        s = jnp.einsum("nld,nmd->nlm", qh, kh, preferred_element_type=F32)
        s = s - jnp.max(s, axis=-1, keepdims=True)
        e = jnp.exp(s)
        pr = (e / jnp.sum(e, axis=-1, keepdims=True)).astype(BF16)
        o = jnp.einsum("nlm,nmd->nld", pr, vh, preferred_element_type=F32)
        part = _dot(o.reshape(m, hd).astype(BF16), wo_ref[cs, :])
        attn = part if attn is None else attn + part
    x2_ref[...] += attn.reshape(ns, l, d)


def _mix_attn(x, sc8, sp16, k3, v3, gmix, win, convw, poolw, pscale, wout, gmem, wq, wo, *, ns, l, pos_base):
    nseq, seq, d = x.shape
    cc = sc8.shape[-1]
    pc = sp16.shape[-1]
    mtok = k3.shape[1]
    seq_tiles = seq // l
    grid = (nseq // ns, seq_tiles)
    const2 = lambda b, j: (0, 0)
    const3 = lambda b, j: (0, 0, 0)
    return pl.pallas_call(
        functools.partial(_mix_attn_kernel, pos_base=pos_base, seq_tiles=seq_tiles),
        out_shape=(jax.ShapeDtypeStruct((nseq, seq, d), F32),
                   jax.ShapeDtypeStruct((nseq, CONV_K - 1, cc), F32),
                   jax.ShapeDtypeStruct((nseq, POOL_BUF, pc), F32)),
        grid=grid,
        in_specs=[pl.BlockSpec((ns, l, d), lambda b, j: (b, j, 0)),
                  pl.BlockSpec((ns, 8, cc), lambda b, j: (b, 0, 0)),
                  pl.BlockSpec((ns, 16, pc), lambda b, j: (b, 0, 0)),
                  pl.BlockSpec((ns, mtok, d), lambda b, j: (b, 0, 0)),
                  pl.BlockSpec((ns, mtok, d), lambda b, j: (b, 0, 0)),
                  pl.BlockSpec(gmix.shape, const2),
                  pl.BlockSpec(win.shape, const2),
                  pl.BlockSpec(convw.shape, const2),
                  pl.BlockSpec(poolw.shape, const3),
                  pl.BlockSpec(pscale.shape, const2),
                  pl.BlockSpec(wout.shape, const2),
                  pl.BlockSpec(gmem.shape, const2),
                  pl.BlockSpec(wq.shape, const2),
                  pl.BlockSpec(wo.shape, const2)],
        out_specs=(pl.BlockSpec((ns, l, d), lambda b, j: (b, j, 0)),
                   pl.BlockSpec((ns, CONV_K - 1, cc), lambda b, j: (b, 0, 0)),
                   pl.BlockSpec((ns, POOL_BUF, pc), lambda b, j: (b, 0, 0))),
        scratch_shapes=[pltpu.VMEM((ns, 8 + l, cc), F32), pltpu.VMEM((ns, 16 + l, pc), F32)],
        compiler_params=pltpu.CompilerParams(dimension_semantics=("arbitrary", "arbitrary"),
                                             vmem_limit_bytes=VMEM_LIMIT_BYTES),
        name="mix_attn",
    )(x, sc8, sp16, k3, v3, gmix, win, convw, poolw, pscale, wout, gmem, wq, wo)


def _extract_sorted(s, vals_ref, rounds):
    rk = jnp.full(s.shape, UNRANKED, F32)
    cur = s
    for r in range(1, rounds + 1):
        mx = jnp.max(cur, axis=0, keepdims=True)
        hit = cur == mx
        rk = jnp.where(hit, float(r), rk)
        cur = jnp.where(hit, -jnp.inf, cur)
        vals_ref[r - 1:r, :] = mx
    return rk


def _route_kernel(x_ref, g_ref, wqt_ref, keys_ref,
                  tt_ref, rk1_ref, e1_ref, n_ref, e0_ref,
                  qt_scr, s_scr, v0_scr, v1_scr, cand_scr):
    tm, d = x_ref.shape
    nk = keys_ref.shape[1]
    half = keys_ref.shape[2]
    t = _rmsnorm(x_ref[...], g_ref[...])
    tt = t.T.astype(BF16)
    tt_ref[...] = tt
    qt_scr[...] = _dot(wqt_ref[...], tt)
    vrows = v0_scr.shape[0]
    v0_scr[...] = jnp.full(v0_scr.shape, -jnp.inf, F32)
    v1_scr[...] = jnp.full(v1_scr.shape, -jnp.inf, F32)

    def head_body(hh, carry):
        for p in range(2):
            row0 = pl.multiple_of((hh * 2 + p) * half, half)
            qblk = qt_scr[pl.ds(row0, half), :].astype(BF16)
            s_scr[p] = _dot(keys_ref[p], qblk)
        for lc in range(tm // LANES):
            ls = slice(lc * LANES, (lc + 1) * LANES)
            s0 = s_scr[0, :, ls]
            s1 = s_scr[1, :, ls]
            rk0 = _extract_sorted(s0, v0_scr, ROUTE_ROUNDS)
            rk1 = _extract_sorted(s1, v1_scr, ROUTE_ROUNDS)
            v0 = v0_scr[...]
            v1 = v1_scr[...]
            cand_scr[0:vrows, :] = v0[0:1, :] + v1
            for a in range(2, ROUTE_ROUNDS + 1):
                r0 = vrows + (a - 2) * SUBLANES
                cand_scr[r0:r0 + SUBLANES, :] = v0[a - 1:a, :] + v1[0:SUBLANES, :]
            cand = cand_scr[...]
            cur = cand
            kth = None
            for r in range(PEER_TOPK):
                kth = jnp.max(cur, axis=0, keepdims=True)
                if r + 1 < PEER_TOPK:
                    cur = jnp.where(cur == kth, -jnp.inf, cur)
            top = v0[0:1, :] + v1[0:1, :]
            sel = cand >= kth
            ex = jnp.where(sel, jnp.exp(cand - top), 0.0)
            zsum = jnp.sum(ex, axis=0, keepdims=True)
            self32 = sel.astype(F32)
            cnt = jnp.zeros(s0.shape, F32)
            n_a = jnp.sum(self32[0:vrows, :], axis=0, keepdims=True)
            cnt = jnp.where(rk0 == 1.0, n_a, cnt)
            for a in range(2, ROUTE_ROUNDS + 1):
                r0 = vrows + (a - 2) * SUBLANES
                n_a = jnp.sum(self32[r0:r0 + SUBLANES, :], axis=0, keepdims=True)
                cnt = jnp.where(rk0 == float(a), n_a, cnt)
            n_ref[hh, :, ls] = cnt
            e0_ref[hh, :, ls] = jnp.exp(s0 - v0[0:1, :])
            e1_ref[hh, :, ls] = (jnp.exp(s1 - v1[0:1, :]) * (0.5 / zsum)).astype(BF16)
            rk1_ref[hh, :, ls] = rk1.astype(BF16)
        return carry

    lax.fori_loop(0, PEER_HEADS, head_body, 0)


def _route(x2d, g, wqt, keys, tm):
    t, d = x2d.shape
    nk = keys.shape[1]
    qrows = wqt.shape[0]
    vrows = ((ROUTE_ROUNDS + SUBLANES - 1) // SUBLANES) * SUBLANES
    cand_rows = vrows + (ROUTE_ROUNDS - 1) * SUBLANES
    hshape = (PEER_HEADS, nk, t)
    hblock = pl.BlockSpec((PEER_HEADS, nk, tm), lambda i: (0, 0, i))
    return pl.pallas_call(
        _route_kernel,
        out_shape=(jax.ShapeDtypeStruct((d, t), BF16),
                   jax.ShapeDtypeStruct(hshape, BF16),
                   jax.ShapeDtypeStruct(hshape, BF16),
                   jax.ShapeDtypeStruct(hshape, F32),
                   jax.ShapeDtypeStruct(hshape, F32)),
        grid=(t // tm,),
        in_specs=[pl.BlockSpec((tm, d), lambda i: (i, 0)),
                  pl.BlockSpec((1, d), lambda i: (0, 0)),
                  pl.BlockSpec(wqt.shape, lambda i: (0, 0)),
                  pl.BlockSpec(keys.shape, lambda i: (0, 0, 0))],
        out_specs=(pl.BlockSpec((d, tm), lambda i: (0, i)), hblock, hblock, hblock, hblock),
        scratch_shapes=[pltpu.VMEM((qrows, tm), F32),
                        pltpu.VMEM((2, nk, tm), F32),
                        pltpu.VMEM((vrows, LANES), F32),
                        pltpu.VMEM((vrows, LANES), F32),
                        pltpu.VMEM((cand_rows, LANES), F32)],
        compiler_params=pltpu.CompilerParams(dimension_semantics=("arbitrary",),
                                             vmem_limit_bytes=VMEM_LIMIT_BYTES),
        name="peer_route",
    )(x2d, g, wqt, keys)


def _expert_kernel(tt_ref, u_ref, vt_ref, rk1_ref, e1_ref, n_ref, e0_ref, x_ref, g_ref,
                   o_ref, at_scr, wt_scr, yt_scr):
    c = pl.program_id(1)
    nchunks = pl.num_programs(1)
    tm = tt_ref.shape[1]
    nk = rk1_ref.shape[1]

    @pl.when(c == 0)
    def _():
        yt_scr[...] = jnp.zeros(yt_scr.shape, F32)

    at_scr[...] = _dot(u_ref[...], tt_ref[...])

    for lc in range(tm // LANES):
        ls = slice(lc * LANES, (lc + 1) * LANES)
        for il in range(SUBLANES):
            gate = jnp.zeros((nk, LANES), BF16)
            for hh in range(PEER_HEADS):
                nb = n_ref[hh, il:il + 1, ls].astype(BF16)
                e0b = e0_ref[hh, il:il + 1, ls].astype(BF16)
                val = e1_ref[hh, :, ls] * e0b
                gate = gate + jnp.where(rk1_ref[hh, :, ls] <= nb, val, jnp.zeros_like(val))
            a = at_scr[il * nk:(il + 1) * nk, ls]
            act = (a * (1.0 + lax.erf(a * (2.0 ** -0.5)))).astype(BF16)
            wt_scr[il * nk:(il + 1) * nk, ls] = gate * act
    yt_scr[...] += _dot(vt_ref[...], wt_scr[...])

    @pl.when(c == nchunks - 1)
    def _():
        o_ref[...] = _rmsnorm(x_ref[...] + yt_scr[...].T, g_ref[...])


def _experts(tt, u, vt, rk1, e1, n, e0, x2d, g, tm):
    d, t = tt.shape
    ne = u.shape[0]
    nk = rk1.shape[1]
    ec = SUBLANES * nk
    hblock = pl.BlockSpec((PEER_HEADS, nk, tm), lambda i, c: (0, 0, i))
    rowblock = pl.BlockSpec((PEER_HEADS, SUBLANES, tm), lambda i, c: (0, c, i))
    return pl.pallas_call(
        _expert_kernel,
        out_shape=jax.ShapeDtypeStruct((t, d), F32),
        grid=(t // tm, ne // ec),
        in_specs=[pl.BlockSpec((d, tm), lambda i, c: (0, i)),
                  pl.BlockSpec((ec, d), lambda i, c: (c, 0)),
                  pl.BlockSpec((d, ec), lambda i, c: (0, c)),
                  hblock, hblock, rowblock, rowblock,
                  pl.BlockSpec((tm, d), lambda i, c: (i, 0)),
                  pl.BlockSpec((1, d), lambda i, c: (0, 0))],
        out_specs=pl.BlockSpec((tm, d), lambda i, c: (i, 0)),
        scratch_shapes=[pltpu.VMEM((ec, tm), F32),
                        pltpu.VMEM((ec, tm), BF16),
                        pltpu.VMEM((d, tm), F32)],
        compiler_params=pltpu.CompilerParams(dimension_semantics=("arbitrary", "arbitrary"),
                                             vmem_limit_bytes=VMEM_LIMIT_BYTES),
        name="peer_experts",
    )(tt, u, vt, rk1, e1, n, e0, x2d, g)


def _pick_tile(n, pref):
    t = min(n, pref)
    while n % t:
        t //= 2
    return t


def _layer_group(x, state_conv, state_pool, k3, v3, pos_base, ns, lp, tail):
    nseq, seq, d = x.shape
    l = _pick_tile(seq, 512)
    sc8 = jnp.pad(state_conv, ((0, 0), (8 - (CONV_K - 1), 0), (0, 0)))
    sp16 = jnp.pad(state_pool, ((0, 0), (16 - POOL_BUF, 0), (0, 0)))
    x2, nconv, npool = _mix_attn(x, sc8, sp16, k3, v3, *lp["mix"], ns=ns, l=l, pos_base=pos_base)
    x2d = x2.reshape(nseq * seq, d)
    t = nseq * seq
    tm = _pick_tile(t, 512)
    tt, rk1, e1, n, e0 = _route(x2d, lp["g_ffn"], lp["wqt"], lp["keys"], _pick_tile(t, 256))
    y = _experts(tt, lp["u"], lp["vt"], rk1, e1, n, e0, x2d, tail, tm)
    return y.reshape(nseq, seq, d), nconv, npool


def kernel(x_prompt, x_sample, mem_prompt, state_conv, state_pool, cache_mem_k, cache_mem_v, norm_mix_g, w_in, conv_w, pool_w, pool_scale, w_out, norm_mem_g, mem_norm_g, w_mq, w_mk, w_mv, w_mo, norm_ffn_g, peer_wq, peer_keys, peer_u, peer_v, final_g):
    depth = w_in.shape[0]
    assert depth == 1, "the final RMSNorm is fused into the last layer's expert kernel"
    nb, mtok, d = mem_prompt.shape
    nsamp = x_sample.shape[0]
    cc = conv_w.shape[-1]
    pc = pool_scale.shape[-1]
    row = lambda v: v.reshape(1, -1)
    tail = row(final_g)

    xp, xs = x_prompt, x_sample
    conv_p, pool_p, mk_p, mv_p, conv_s, pool_s = [], [], [], [], [], []
    for li in range(depth):
        lp = {
            "mix": (row(norm_mix_g[li]), w_in[li].astype(BF16), conv_w[li], pool_w[li].astype(BF16),
                    row(pool_scale[li]), w_out[li].astype(BF16), row(norm_mem_g[li]),
                    w_mq[li].astype(BF16), w_mo[li].astype(BF16)),
            "g_ffn": row(norm_ffn_g[li]),
            "wqt": peer_wq[li].T.astype(BF16),
            "keys": peer_keys[li].astype(BF16),
            "u": peer_u[li].astype(BF16),
            "vt": peer_v[li].T.astype(BF16),
        }
        k2, v2 = _mem_kv(mem_prompt.reshape(nb * mtok, d), row(mem_norm_g[li]),
                         w_mk[li].astype(BF16), w_mv[li].astype(BF16), _pick_tile(nb * mtok, 512))
        zc = jnp.zeros((nb, CONV_K - 1, cc), F32)
        zp = jnp.zeros((nb, POOL_BUF, pc), F32)
        xp, nc_p, np_p = _layer_group(xp, zc, zp, k2.reshape(nb, mtok, d), v2.reshape(nb, mtok, d),
                                      0, 1, lp, tail)
        xs, nc_s, np_s = _layer_group(xs, state_conv[li], state_pool[li],
                                      cache_mem_k[li].reshape(nsamp, mtok, d),
                                      cache_mem_v[li].reshape(nsamp, mtok, d),
                                      PAST_LEN, _pick_tile(nsamp, 4), lp, tail)
        conv_p.append(nc_p)
        pool_p.append(np_p)
        mk_p.append(k2.reshape(nb, mtok, MEM_HEADS, d // MEM_HEADS))
        mv_p.append(v2.reshape(nb, mtok, MEM_HEADS, d // MEM_HEADS))
        conv_s.append(nc_s)
        pool_s.append(np_s)
    return (xp, xs, jnp.stack(conv_p), jnp.stack(pool_p), jnp.stack(mk_p), jnp.stack(mv_p),
            jnp.stack(conv_s), jnp.stack(pool_s))
```
